```python
import jax, jax.numpy as jnp
from jax import lax
import numpy as np

D_MODEL = 1024
BATCH = 16
SEQ = 2048
DEPTH = 2

N_Q_HEADS = 8
N_KV_HEADS = 2
HEAD_DIM = 64
Q_GROUP = N_Q_HEADS // N_KV_HEADS
ATT_WIDTH = N_Q_HEADS * HEAD_DIM
KV_WIDTH = N_KV_HEADS * HEAD_DIM
WINDOW = 128
ATT_BLOCK = 128
SGU_WIDTH = D_MODEL // 2
SGU_GROUPS = 8
SGU_GROUP_DIM = SGU_WIDTH // SGU_GROUPS
SGU_CHUNK = 128
N_BRANCHES = 2
IN_WIDTH = ATT_WIDTH + 2 * KV_WIDTH + 2 * SGU_WIDTH + N_BRANCHES * D_MODEL
D_FF = 256 * ((8 * D_MODEL // 3 + 255) // 256)
CONV_WIDTH = 3
NORM_EPS = 1e-6
NEG_INF = -1e30

kernel_name = "hybrid_gated_swa_sgu_convffn"


def rmsnorm(x, gain):
    xf = x.astype(jnp.float32)
    y = xf * lax.rsqrt(jnp.mean(xf * xf, axis=-1, keepdims=True) + NORM_EPS)
    return (y * gain.astype(jnp.float32)).astype(x.dtype)


def alibi_slopes():
    return jnp.exp2(-8.0 * (jnp.arange(N_Q_HEADS, dtype=jnp.float32) + 1.0) / N_Q_HEADS)


def sliding_window_attention(q, k, v, q_gain, k_gain, sinks):
    B, S, _ = q.shape
    nb = S // ATT_BLOCK
    q = rmsnorm(q.reshape(B, S, N_Q_HEADS, HEAD_DIM), q_gain)
    k = rmsnorm(k.reshape(B, S, N_KV_HEADS, HEAD_DIM), k_gain)
    v = v.reshape(B, S, N_KV_HEADS, HEAD_DIM)
    qb = q.reshape(B, nb, ATT_BLOCK, N_KV_HEADS, Q_GROUP, HEAD_DIM)

    def band(t):
        tp = jnp.pad(t, ((0, 0), (ATT_BLOCK, 0), (0, 0), (0, 0)))
        tp = tp.reshape(B, nb + 1, ATT_BLOCK, N_KV_HEADS, HEAD_DIM)
        return jnp.concatenate([tp[:, :-1], tp[:, 1:]], axis=2)

    kb, vb = band(k), band(v)
    scores = jnp.einsum('bnqhgd,bnkhd->bnhgqk', qb, kb,
                        preferred_element_type=jnp.float32) * (HEAD_DIM ** -0.5)
    qi = jnp.arange(ATT_BLOCK)[:, None]
    kj = jnp.arange(2 * ATT_BLOCK)[None, :]
    dist = qi + ATT_BLOCK - kj
    key_pos = jnp.arange(nb)[:, None] * ATT_BLOCK - ATT_BLOCK + kj
    valid = ((dist >= 0) & (dist < WINDOW))[None] & (key_pos >= 0)[:, None, :]
    slopes = alibi_slopes().reshape(N_KV_HEADS, Q_GROUP)
    alibi = -slopes[:, :, None, None] * dist.astype(jnp.float32)[None, None]
    scores = jnp.where(valid[None, :, None, None], scores + alibi[None, None], NEG_INF)
    sink = jnp.broadcast_to(
        sinks.astype(jnp.float32).reshape(1, 1, N_KV_HEADS, Q_GROUP, 1, 1),
        scores.shape[:-1] + (1,))
    probs = jax.nn.softmax(jnp.concatenate([scores, sink], axis=-1), axis=-1)[..., :-1]
    out = jnp.einsum('bnhgqk,bnkhd->bnqhgd', probs.astype(v.dtype), vb)
    return out.reshape(B, S, ATT_WIDTH)


def chunked_spatial_gating(u, v, sgu_gain, w_s, b_s):
    B, S, _ = u.shape
    nc = S // SGU_CHUNK
    u = jax.nn.gelu(u)
    v = rmsnorm(jax.nn.gelu(v), sgu_gain)
    vc = v.reshape(B, nc, SGU_CHUNK, SGU_GROUPS, SGU_GROUP_DIM)
    causal = jnp.tril(jnp.ones((SGU_CHUNK, SGU_CHUNK), dtype=bool))
    w = jnp.where(causal[None], w_s, 0)
    mixed = jnp.einsum('gij,bcjgd->bcigd', w, vc) + b_s.T[:, :, None]
    return u * mixed.reshape(B, S, SGU_WIDTH)


def conv_gated_mlp(h, w_up, conv_w, conv_b, w_down):
    S = h.shape[1]
    z = h @ w_up
    zp = jnp.pad(z, ((0, 0), (CONV_WIDTH - 1, 0), (0, 0)))
    zc = conv_b
    for tap in range(CONV_WIDTH):
        zc = zc + conv_w[tap] * zp[:, tap:tap + S]
    gate, val = jnp.split(zc, 2, axis=-1)
    return (jax.nn.silu(gate) * val) @ w_down


def setup_inputs(seed: int = 0) -> dict:
    key = jax.random.key(seed)
    ks = jax.random.split(key, 20)
    f32 = jnp.float32

    def nrm(k, shape, scale):
        return jax.random.normal(k, shape, f32) * scale

    return {
        "x": nrm(ks[0], (BATCH, SEQ, D_MODEL), 1.0),
        "mix_norm": 1.0 + nrm(ks[1], (DEPTH, D_MODEL), 0.05),
        "w_in": nrm(ks[2], (DEPTH, D_MODEL, IN_WIDTH), D_MODEL ** -0.5),
        "q_norm": 1.0 + nrm(ks[3], (DEPTH, HEAD_DIM), 0.05),
        "k_norm": 1.0 + nrm(ks[4], (DEPTH, HEAD_DIM), 0.05),
        "sinks": nrm(ks[5], (DEPTH, N_Q_HEADS), 0.5),
        "sgu_norm": 1.0 + nrm(ks[6], (DEPTH, SGU_WIDTH), 0.05),
        "w_s": nrm(ks[7], (DEPTH, SGU_GROUPS, SGU_CHUNK, SGU_CHUNK), SGU_CHUNK ** -0.5),
        "b_s": 1.0 + nrm(ks[8], (DEPTH, SGU_GROUPS, SGU_CHUNK), 0.1),
        "w_oa": nrm(ks[9], (DEPTH, ATT_WIDTH, D_MODEL), ATT_WIDTH ** -0.5),
        "w_ob": nrm(ks[10], (DEPTH, SGU_WIDTH, D_MODEL), SGU_WIDTH ** -0.5),
        "w_out": nrm(ks[11], (DEPTH, D_MODEL, D_MODEL), D_MODEL ** -0.5),
        "ffn_norm": 1.0 + nrm(ks[12], (DEPTH, D_MODEL), 0.05),
        "w_up": nrm(ks[13], (DEPTH, D_MODEL, 2 * D_FF), D_MODEL ** -0.5),
        "conv_w": nrm(ks[14], (DEPTH, CONV_WIDTH, 2 * D_FF), CONV_WIDTH ** -0.5),
        "conv_b": nrm(ks[15], (DEPTH, 2 * D_FF), 0.02),
        "w_down": nrm(ks[16], (DEPTH, D_FF, D_MODEL), D_FF ** -0.5),
    }


def reference(x, mix_norm, w_in, q_norm, k_norm, sinks, sgu_norm, w_s, b_s,
              w_oa, w_ob, w_out, ffn_norm, w_up, conv_w, conv_b, w_down):
    splits = (ATT_WIDTH,
              ATT_WIDTH + KV_WIDTH,
              ATT_WIDTH + 2 * KV_WIDTH,
              ATT_WIDTH + 2 * KV_WIDTH + SGU_WIDTH,
              ATT_WIDTH + 2 * KV_WIDTH + 2 * SGU_WIDTH,
              ATT_WIDTH + 2 * KV_WIDTH + 2 * SGU_WIDTH + D_MODEL)
    for l in range(DEPTH):
        h = rmsnorm(x, mix_norm[l])
        proj = h @ w_in[l]
        q, k, v, su, sv, g_a, g_b = jnp.split(proj, splits, axis=-1)
        y_att = sliding_window_attention(q, k, v, q_norm[l], k_norm[l], sinks[l])
        y_sgu = chunked_spatial_gating(su, sv, sgu_norm[l], w_s[l], b_s[l])
        merged = (jax.nn.sigmoid(g_a) * (y_att @ w_oa[l])
                  + jax.nn.sigmoid(g_b) * (y_sgu @ w_ob[l]))
        x = x + merged @ w_out[l]
        x = x + conv_gated_mlp(rmsnorm(x, ffn_norm[l]), w_up[l], conv_w[l], conv_b[l], w_down[l])
    return x
```

```python
import functools

import jax
import jax.numpy as jnp
from jax import lax
from jax.experimental import pallas as pl
from jax.experimental.pallas import tpu as pltpu

D_MODEL = 1024
N_Q_HEADS = 8
N_KV_HEADS = 2
HEAD_DIM = 64
ATT_WIDTH = N_Q_HEADS * HEAD_DIM
KV_WIDTH = N_KV_HEADS * HEAD_DIM
BLOCK = 128
SGU_WIDTH = D_MODEL // 2
SGU_GROUPS = 8
IN_WIDTH = ATT_WIDTH + 2 * KV_WIDTH + 2 * SGU_WIDTH + 2 * D_MODEL
D_FF = 2816
CONV_WIDTH = 3
NORM_EPS = 1e-6
NEG_INF = -1e30

LANES = 128
SUBLANES = 8
VMEM_LIMIT_BYTES = 56 * 1024 * 1024

SEQ_TILE = 256
FF_CHUNK = 256

Q_OFF = 0
KV_OFF = ATT_WIDTH
SU_OFF = KV_OFF + 2 * KV_WIDTH
SV_OFF = SU_OFF + SGU_WIDTH
GA_OFF = SV_OFF + SGU_WIDTH
GB_OFF = GA_OFF + D_MODEL

BF16 = jnp.bfloat16
F32 = jnp.float32


def _dot(a, b):
    return jnp.dot(a, b, preferred_element_type=F32)


def _dot_nt(a, b):
    return lax.dot_general(a, b, (((1,), (1,)), ((), ())), preferred_element_type=F32)


def _rmsnorm_rows(x, gain):
    ms = jnp.mean(x * x, axis=-1, keepdims=True)
    return x * lax.rsqrt(ms + NORM_EPS) * gain


def _half_rmsnorm(t, gain2, lo):
    t2 = t * t
    s_lo = jnp.sum(jnp.where(lo, t2, 0.0), axis=-1, keepdims=True)
    s_hi = jnp.sum(jnp.where(lo, 0.0, t2), axis=-1, keepdims=True)
    ms = jnp.where(lo, s_lo, s_hi) * (1.0 / HEAD_DIM)
    return t * lax.rsqrt(ms + NORM_EPS) * gain2


def _mixer_kernel(layer, x_ref, gain_ref, w_in_ref, qg_ref, kg_ref, sinks_ref, sgu_gain_ref,
                  ws_ref, bexp_ref, w_oa_ref, w_ob_ref, w_out_ref, o_ref,
                  kprev_ref, kprev_sw_ref, vprev_ref, vprev_sw_ref):
    s_idx = pl.program_id(1)
    tile = x_ref.shape[0]
    n_blocks = tile // BLOCK

    @pl.when(s_idx == 0)
    def _():
        kprev_ref[...] = jnp.zeros_like(kprev_ref)
        kprev_sw_ref[...] = jnp.zeros_like(kprev_sw_ref)
        vprev_ref[...] = jnp.zeros_like(vprev_ref)
        vprev_sw_ref[...] = jnp.zeros_like(vprev_sw_ref)

    x = x_ref[...]
    h = _rmsnorm_rows(x, gain_ref[...]).astype(BF16)

    lane = lax.broadcasted_iota(jnp.int32, (1, LANES), 1)
    lo = lane < HEAD_DIM

    q = _dot(h, w_in_ref[:, Q_OFF:Q_OFF + ATT_WIDTH])
    kv = _dot(h, w_in_ref[:, KV_OFF:KV_OFF + 2 * KV_WIDTH])
    k = _half_rmsnorm(kv[:, :KV_WIDTH], kg_ref[...], lo)
    v = kv[:, KV_WIDTH:]
    k_sw = pltpu.roll(k, HEAD_DIM, axis=1)
    v_sw = pltpu.roll(v, HEAD_DIM, axis=1)
    k_b, k_sw_b, v_b, v_sw_b = (t.astype(BF16) for t in (k, k_sw, v, v_sw))

    scale = HEAD_DIM ** -0.5
    qg = qg_ref[...] * scale
    q_lo, q_hi = [], []
    for c in range(ATT_WIDTH // LANES):
        qc = _half_rmsnorm(q[:, c * LANES:(c + 1) * LANES], qg, lo)
        q_lo.append(jnp.where(lo, qc, 0.0).astype(BF16))
        q_hi.append(jnp.where(lo, 0.0, qc).astype(BF16))

    qi = lax.broadcasted_iota(jnp.int32, (BLOCK, 2 * BLOCK), 0)
    kj = lax.broadcasted_iota(jnp.int32, (BLOCK, 2 * BLOCK), 1)
    dist = qi + BLOCK - kj
    in_window = (dist >= 0) & (dist < BLOCK)
    dist_f = dist.astype(F32)
    first_col = jnp.where(s_idx > 0, 0, BLOCK)

    y_att_blocks = []
    for n in range(n_blocks):
        rows = slice(n * BLOCK, (n + 1) * BLOCK)
        if n == 0:
            kp, kp_sw, vp, vp_sw = kprev_ref[...], kprev_sw_ref[...], vprev_ref[...], vprev_sw_ref[...]
            valid = in_window & (kj >= first_col)
        else:
            prev = slice((n - 1) * BLOCK, n * BLOCK)
            kp, kp_sw, vp, vp_sw = k_b[prev], k_sw_b[prev], v_b[prev], v_sw_b[prev]
            valid = in_window
        kb = jnp.concatenate([kp, k_b[rows]], axis=0)
        kb_sw = jnp.concatenate([kp_sw, k_sw_b[rows]], axis=0)
        vb = jnp.concatenate([vp, v_b[rows]], axis=0)
        vb_sw = jnp.concatenate([vp_sw, v_sw_b[rows]], axis=0)
        vb_lo = jnp.where(lo, vb, jnp.zeros_like(vb))
        vb_hi = jnp.where(lo, jnp.zeros_like(vb), vb)
        vsw_lo = jnp.where(lo, vb_sw, jnp.zeros_like(vb))
        vsw_hi = jnp.where(lo, jnp.zeros_like(vb), vb_sw)

        q_same = jnp.concatenate([q_lo[0][rows], q_lo[1][rows], q_hi[2][rows], q_hi[3][rows]], axis=0)
        q_swap = jnp.concatenate([q_hi[0][rows], q_hi[1][rows], q_lo[2][rows], q_lo[3][rows]], axis=0)
        s_same = _dot_nt(q_same, kb)
        s_swap = _dot_nt(q_swap, kb_sw)
        head_scores = {0: s_same[0:128], 2: s_same[128:256], 5: s_same[256:384], 7: s_same[384:512],
                       1: s_swap[0:128], 3: s_swap[128:256], 4: s_swap[256:384], 6: s_swap[384:512]}

        probs, inv = {}, {}
        for hd in range(N_Q_HEADS):
            slope = 2.0 ** (-(hd + 1))
            sc = jnp.where(valid, head_scores[hd] - slope * dist_f, NEG_INF)
            sink = sinks_ref[layer, hd]
            m = jnp.maximum(jnp.max(sc, axis=-1, keepdims=True), sink)
            p = jnp.exp(sc - m)
            denom = jnp.sum(p, axis=-1, keepdims=True) + jnp.exp(sink - m)
            probs[hd] = p.astype(BF16)
            inv[hd] = 1.0 / denom

        cols = []
        for c in range(ATT_WIDTH // LANES):
            if c < 2:
                o = _dot(probs[2 * c], vb_lo) + _dot(probs[2 * c + 1], vsw_hi)
            else:
                o = _dot(probs[2 * c], vsw_lo) + _dot(probs[2 * c + 1], vb_hi)
            cols.append(o * jnp.where(lo, inv[2 * c], inv[2 * c + 1]))
        y_att_blocks.append(jnp.concatenate(cols, axis=1))

    kprev_ref[...] = k_b[tile - BLOCK:]
    kprev_sw_ref[...] = k_sw_b[tile - BLOCK:]
    vprev_ref[...] = v_b[tile - BLOCK:]
    vprev_sw_ref[...] = v_sw_b[tile - BLOCK:]
    y_att = jnp.concatenate(y_att_blocks, axis=0).astype(BF16)

    su = _dot(h, w_in_ref[:, SU_OFF:SU_OFF + SGU_WIDTH])
    sv = _dot(h, w_in_ref[:, SV_OFF:SV_OFF + SGU_WIDTH])
    u = jax.nn.gelu(su)
    vn = _rmsnorm_rows(jax.nn.gelu(sv), sgu_gain_ref[...])
    ci = lax.broadcasted_iota(jnp.int32, (BLOCK, BLOCK), 0)
    cj = lax.broadcasted_iota(jnp.int32, (BLOCK, BLOCK), 1)
    causal = cj <= ci
    w_tril = [jnp.where(causal, ws_ref[g], 0.0).astype(BF16) for g in range(SGU_GROUPS)]
    bexp = bexp_ref[...]
    sgu_blocks = []
    for n in range(n_blocks):
        rows = slice(n * BLOCK, (n + 1) * BLOCK)
        cols = []
        for c in range(SGU_WIDTH // LANES):
            vc = vn[rows, c * LANES:(c + 1) * LANES]
            v_lo = jnp.where(lo, vc, 0.0).astype(BF16)
            v_hi = jnp.where(lo, 0.0, vc).astype(BF16)
            cols.append(_dot(w_tril[2 * c], v_lo) + _dot(w_tril[2 * c + 1], v_hi))
        mixed = jnp.concatenate(cols, axis=1) + bexp
        sgu_blocks.append(u[rows] * mixed)
    y_sgu = jnp.concatenate(sgu_blocks, axis=0).astype(BF16)

    ga = _dot(h, w_in_ref[:, GA_OFF:GA_OFF + D_MODEL])
    gb = _dot(h, w_in_ref[:, GB_OFF:GB_OFF + D_MODEL])
    merged = (jax.nn.sigmoid(ga) * _dot(y_att, w_oa_ref[...])
              + jax.nn.sigmoid(gb) * _dot(y_sgu, w_ob_ref[...]))
    o_ref[...] = x + _dot(merged.astype(BF16), w_out_ref[...])


def _ffn_kernel(x_ref, gain_ref, w_up_ref, cw_ref, cb_ref, w_down_ref, o_ref, zprev_ref):
    s_idx = pl.program_id(1)
    tile = x_ref.shape[0]

    @pl.when(s_idx == 0)
    def _():
        zprev_ref[...] = jnp.zeros_like(zprev_ref)

    x = x_ref[...]
    h = _rmsnorm_rows(x, gain_ref[...]).astype(BF16)
    row8 = lax.broadcasted_iota(jnp.int32, (SUBLANES, 1), 0)

    def conv(col0):
        cols = slice(col0, col0 + FF_CHUNK)
        z = _dot(h, w_up_ref[:, cols])
        zp = zprev_ref[:, cols]
        zprev_ref[:, cols] = z[tile - SUBLANES:]
        z1 = pltpu.roll(z, 1, axis=0)
        z2 = pltpu.roll(z, 2, axis=0)
        top1 = jnp.where(row8 < 1, pltpu.roll(zp, 1, axis=0), z1[:SUBLANES])
        top2 = jnp.where(row8 < 2, pltpu.roll(zp, 2, axis=0), z2[:SUBLANES])
        z1 = jnp.concatenate([top1, z1[SUBLANES:]], axis=0)
        z2 = jnp.concatenate([top2, z2[SUBLANES:]], axis=0)
        cw = cw_ref[:, cols]
        return cb_ref[:, cols] + cw[0:1] * z2 + cw[1:2] * z1 + cw[2:3] * z

    acc = x
    for j in range(D_FF // FF_CHUNK):
        gate = conv(j * FF_CHUNK)
        val = conv(D_FF + j * FF_CHUNK)
        act = (jax.nn.silu(gate) * val).astype(BF16)
        acc = acc + _dot(act, w_down_ref[j * FF_CHUNK:(j + 1) * FF_CHUNK, :])
    o_ref[...] = acc


def _resident(shape, index_map):
    return pl.BlockSpec(shape, index_map, pipeline_mode=pl.Buffered(1))


def _mixer_call(layer, x, mix_norm, w_in, qg2, kg2, sinks, sgu_norm, w_s, b_exp, w_oa, w_ob, w_out):
    batch, seq, _ = x.shape
    grid = (batch, seq // SEQ_TILE)
    x_spec = pl.BlockSpec((None, SEQ_TILE, D_MODEL), lambda b, s: (b, s, 0))
    const3 = lambda b, s: (layer, 0, 0)
    const4 = lambda b, s: (layer, 0, 0, 0)
    return pl.pallas_call(
        functools.partial(_mixer_kernel, layer),
        grid=grid,
        in_specs=[
            x_spec,
            _resident((None, 1, D_MODEL), const3),
            _resident((None, D_MODEL, IN_WIDTH), const3),
            _resident((None, 1, LANES), const3),
            _resident((None, 1, LANES), const3),
            pl.BlockSpec(memory_space=pltpu.SMEM),
            _resident((None, 1, SGU_WIDTH), const3),
            _resident((None, SGU_GROUPS, BLOCK, BLOCK), const4),
            _resident((None, BLOCK, SGU_WIDTH), const3),
            _resident((None, ATT_WIDTH, D_MODEL), const3),
            _resident((None, SGU_WIDTH, D_MODEL), const3),
            _resident((None, D_MODEL, D_MODEL), const3),
        ],
        out_specs=x_spec,
        out_shape=jax.ShapeDtypeStruct(x.shape, x.dtype),
        scratch_shapes=[pltpu.VMEM((BLOCK, LANES), BF16)] * 4,
        compiler_params=pltpu.CompilerParams(
            dimension_semantics=("parallel", "arbitrary"),
            vmem_limit_bytes=VMEM_LIMIT_BYTES),
        name=f"mixer_l{layer}",
    )(x, mix_norm, w_in, qg2, kg2, sinks, sgu_norm, w_s, b_exp, w_oa, w_ob, w_out)


def _ffn_call(layer, x, ffn_norm, w_up, conv_w, conv_b, w_down):
    batch, seq, _ = x.shape
    grid = (batch, seq // SEQ_TILE)
    x_spec = pl.BlockSpec((None, SEQ_TILE, D_MODEL), lambda b, s: (b, s, 0))
    const3 = lambda b, s: (layer, 0, 0)
    return pl.pallas_call(
        _ffn_kernel,
        grid=grid,
        in_specs=[
            x_spec,
            _resident((None, 1, D_MODEL), const3),
            _resident((None, D_MODEL, 2 * D_FF), const3),
            _resident((None, CONV_WIDTH, 2 * D_FF), const3),
            _resident((None, 1, 2 * D_FF), const3),
            _resident((None, D_FF, D_MODEL), const3),
        ],
        out_specs=x_spec,
        out_shape=jax.ShapeDtypeStruct(x.shape, x.dtype),
        scratch_shapes=[pltpu.VMEM((SUBLANES, 2 * D_FF), F32)],
        compiler_params=pltpu.CompilerParams(
            dimension_semantics=("parallel", "arbitrary"),
            vmem_limit_bytes=VMEM_LIMIT_BYTES),
        name=f"ffn_l{layer}",
    )(x, ffn_norm, w_up, conv_w, conv_b, w_down)


def kernel(x, mix_norm, w_in, q_norm, k_norm, sinks, sgu_norm, w_s, b_s, w_oa, w_ob, w_out,
           ffn_norm, w_up, conv_w, conv_b, w_down):
    depth = w_in.shape[0]
    assert x.shape[1] % SEQ_TILE == 0 and SEQ_TILE % BLOCK == 0 and D_FF % FF_CHUNK == 0
    w_in_b, w_oa_b, w_ob_b, w_out_b, w_up_b, w_down_b = (
        w.astype(BF16) for w in (w_in, w_oa, w_ob, w_out, w_up, w_down))
    mix_norm3 = mix_norm[:, None, :]
    ffn_norm3 = ffn_norm[:, None, :]
    sgu_norm3 = sgu_norm[:, None, :]
    conv_b3 = conv_b[:, None, :]
    qg2 = jnp.tile(q_norm, (1, 2))[:, None, :]
    kg2 = jnp.tile(k_norm, (1, 2))[:, None, :]
    b_exp = jnp.repeat(jnp.swapaxes(b_s, 1, 2), SGU_WIDTH // SGU_GROUPS, axis=2)
    for layer in range(depth):
        x = _mixer_call(layer, x, mix_norm3, w_in_b, qg2, kg2, sinks, sgu_norm3, w_s, b_exp,
                        w_oa_b, w_ob_b, w_out_b)
        x = _ffn_call(layer, x, ffn_norm3, w_up_b, conv_w, conv_b3, w_down_b)
    return x
```

```python
import functools

import jax
import jax.numpy as jnp
from jax import lax
from jax.experimental import pallas as pl
from jax.experimental.pallas import tpu as pltpu

D_MODEL = 1024
N_Q_HEADS = 8
N_KV_HEADS = 2
HEAD_DIM = 64
ATT_WIDTH = N_Q_HEADS * HEAD_DIM
KV_WIDTH = N_KV_HEADS * HEAD_DIM
BLOCK = 128
SGU_WIDTH = D_MODEL // 2
SGU_GROUPS = 8
IN_WIDTH = ATT_WIDTH + 2 * KV_WIDTH + 2 * SGU_WIDTH + 2 * D_MODEL
D_FF = 2816
CONV_WIDTH = 3
NORM_EPS = 1e-6
NEG_INF = -1e30

LANES = 128
SUBLANES = 8
VMEM_LIMIT_BYTES = 56 * 1024 * 1024

SEQ_TILE = 256
FF_CHUNK = 256

Q_OFF = 0
KV_OFF = ATT_WIDTH
SU_OFF = KV_OFF + 2 * KV_WIDTH
SV_OFF = SU_OFF + SGU_WIDTH
GA_OFF = SV_OFF + SGU_WIDTH
GB_OFF = GA_OFF + D_MODEL

BF16 = jnp.bfloat16
F32 = jnp.float32


def _dot(a, b):
    return jnp.dot(a, b, preferred_element_type=F32)


def _dot_nt(a, b):
    return lax.dot_general(a, b, (((1,), (1,)), ((), ())), preferred_element_type=F32)


def _rmsnorm_rows(x, gain):
    ms = jnp.mean(x * x, axis=-1, keepdims=True)
    return x * lax.rsqrt(ms + NORM_EPS) * gain


def _half_rmsnorm(t, gain2, lo):
    t2 = t * t
    s_lo = jnp.sum(jnp.where(lo, t2, 0.0), axis=-1, keepdims=True)
    s_hi = jnp.sum(jnp.where(lo, 0.0, t2), axis=-1, keepdims=True)
    ms = jnp.where(lo, s_lo, s_hi) * (1.0 / HEAD_DIM)
    return t * lax.rsqrt(ms + NORM_EPS) * gain2


def _mixer_kernel(layer, x_ref, gain_ref, w_in_ref, qg_ref, kg_ref, sinks_ref, sgu_gain_ref,
                  ws_ref, bexp_ref, w_oa_ref, w_ob_ref, w_out_ref, o_ref,
                  kprev_ref, kprev_sw_ref, vprev_ref, vprev_sw_ref):
    s_idx = pl.program_id(1)
    tile = x_ref.shape[0]
    n_blocks = tile // BLOCK

    @pl.when(s_idx == 0)
    def _():
        kprev_ref[...] = jnp.zeros_like(kprev_ref)
        kprev_sw_ref[...] = jnp.zeros_like(kprev_sw_ref)
        vprev_ref[...] = jnp.zeros_like(vprev_ref)
        vprev_sw_ref[...] = jnp.zeros_like(vprev_sw_ref)

    x = x_ref[...]
    h = _rmsnorm_rows(x, gain_ref[...]).astype(BF16)

    lane = lax.broadcasted_iota(jnp.int32, (1, LANES), 1)
    lo = lane < HEAD_DIM

    def proj(off, width):
        return _dot(h, w_in_ref[:, off:off + width])

    q = proj(Q_OFF, ATT_WIDTH)
    kv = proj(KV_OFF, 2 * KV_WIDTH)
    su = proj(SU_OFF, SGU_WIDTH)
    sv = proj(SV_OFF, SGU_WIDTH)

    k = _half_rmsnorm(kv[:, :KV_WIDTH], kg_ref[...], lo)
    v = kv[:, KV_WIDTH:]
    k_sw = pltpu.roll(k, HEAD_DIM, axis=1)
    v_sw = pltpu.roll(v, HEAD_DIM, axis=1)
    k_b, k_sw_b, v_b, v_sw_b = (t.astype(BF16) for t in (k, k_sw, v, v_sw))

    scale = HEAD_DIM ** -0.5
    qg = qg_ref[...] * scale
    q_lo, q_hi = [], []
    for c in range(ATT_WIDTH // LANES):
        qc = _half_rmsnorm(q[:, c * LANES:(c + 1) * LANES], qg, lo)
        q_lo.append(jnp.where(lo, qc, 0.0).astype(BF16))
        q_hi.append(jnp.where(lo, 0.0, qc).astype(BF16))

    qi = lax.broadcasted_iota(jnp.int32, (BLOCK, 2 * BLOCK), 0)
    kj = lax.broadcasted_iota(jnp.int32, (BLOCK, 2 * BLOCK), 1)
    dist = qi + BLOCK - kj
    in_window = (dist >= 0) & (dist < BLOCK)
    dist_f = dist.astype(F32)
    first_col = jnp.where(s_idx > 0, 0, BLOCK)

    def band(cur, prev_ref, n):
        prev = prev_ref[...] if n == 0 else cur[(n - 1) * BLOCK:n * BLOCK]
        return jnp.concatenate([prev, cur[n * BLOCK:(n + 1) * BLOCK]], axis=0)

    def qk(n):
        rows = slice(n * BLOCK, (n + 1) * BLOCK)
        q_same = jnp.concatenate([q_lo[0][rows], q_lo[1][rows], q_hi[2][rows], q_hi[3][rows]], axis=0)
        q_swap = jnp.concatenate([q_hi[0][rows], q_hi[1][rows], q_lo[2][rows], q_lo[3][rows]], axis=0)
        s_same = _dot_nt(q_same, band(k_b, kprev_ref, n))
        s_swap = _dot_nt(q_swap, band(k_sw_b, kprev_sw_ref, n))
        return {0: s_same[0:128], 2: s_same[128:256], 5: s_same[256:384], 7: s_same[384:512],
                1: s_swap[0:128], 3: s_swap[128:256], 4: s_swap[256:384], 6: s_swap[384:512]}

    def softmax(n, head_scores):
        valid = in_window & (kj >= first_col) if n == 0 else in_window
        probs, inv = {}, {}
        for hd in range(N_Q_HEADS):
            slope = 2.0 ** (-(hd + 1))
            sc = jnp.where(valid, head_scores[hd] - slope * dist_f, NEG_INF)
            sink = sinks_ref[layer, hd]
            m = jnp.maximum(jnp.max(sc, axis=-1, keepdims=True), sink)
            p = jnp.exp(sc - m)
            denom = jnp.sum(p, axis=-1, keepdims=True) + jnp.exp(sink - m)
            probs[hd] = p.astype(BF16)
            inv[hd] = 1.0 / denom
        return probs, inv

    def pv(n, probs, inv):
        vb = band(v_b, vprev_ref, n)
        vb_sw = band(v_sw_b, vprev_sw_ref, n)
        zero = jnp.zeros_like(vb)
        vb_lo, vb_hi = jnp.where(lo, vb, zero), jnp.where(lo, zero, vb)
        vsw_lo, vsw_hi = jnp.where(lo, vb_sw, zero), jnp.where(lo, zero, vb_sw)
        cols = []
        for c in range(ATT_WIDTH // LANES):
            if c < 2:
                o = _dot(probs[2 * c], vb_lo) + _dot(probs[2 * c + 1], vsw_hi)
            else:
                o = _dot(probs[2 * c], vsw_lo) + _dot(probs[2 * c + 1], vb_hi)
            cols.append(o * jnp.where(lo, inv[2 * c], inv[2 * c + 1]))
        return jnp.concatenate(cols, axis=1)

    gate_proj = [GA_OFF, GB_OFF]
    gates = []
    scores = {0: qk(0)}
    soft = {}
    y_att_blocks = []
    for n in range(n_blocks):
        if n + 1 < n_blocks:
            scores[n + 1] = qk(n + 1)
        if len(gates) < len(gate_proj):
            gates.append(proj(gate_proj[len(gates)], D_MODEL))
        if n > 0:
            y_att_blocks.append(pv(n - 1, *soft.pop(n - 1)))
        soft[n] = softmax(n, scores.pop(n))
    while len(gates) < len(gate_proj):
        gates.append(proj(gate_proj[len(gates)], D_MODEL))
    ga, gb = gates
    y_att_blocks.append(pv(n_blocks - 1, *soft.pop(n_blocks - 1)))

    kprev_ref[...] = k_b[tile - BLOCK:]
    kprev_sw_ref[...] = k_sw_b[tile - BLOCK:]
    vprev_ref[...] = v_b[tile - BLOCK:]
    vprev_sw_ref[...] = v_sw_b[tile - BLOCK:]
    y_att = jnp.concatenate(y_att_blocks, axis=0).astype(BF16)

    u = jax.nn.gelu(su)
    vn = _rmsnorm_rows(jax.nn.gelu(sv), sgu_gain_ref[...])
    ci = lax.broadcasted_iota(jnp.int32, (BLOCK, BLOCK), 0)
    cj = lax.broadcasted_iota(jnp.int32, (BLOCK, BLOCK), 1)
    causal = cj <= ci
    w_tril = [jnp.where(causal, ws_ref[g], 0.0).astype(BF16) for g in range(SGU_GROUPS)]
    bexp = bexp_ref[...]
    sgu_blocks = []
    for n in range(n_blocks):
        rows = slice(n * BLOCK, (n + 1) * BLOCK)
        cols = []
        for c in range(SGU_WIDTH // LANES):
            vc = vn[rows, c * LANES:(c + 1) * LANES]
            v_lo = jnp.where(lo, vc, 0.0).astype(BF16)
            v_hi = jnp.where(lo, 0.0, vc).astype(BF16)
            cols.append(_dot(w_tril[2 * c], v_lo) + _dot(w_tril[2 * c + 1], v_hi))
        mixed = jnp.concatenate(cols, axis=1) + bexp
        sgu_blocks.append(u[rows] * mixed)
    y_sgu = jnp.concatenate(sgu_blocks, axis=0).astype(BF16)

    merged = (jax.nn.sigmoid(ga) * _dot(y_att, w_oa_ref[...])
              + jax.nn.sigmoid(gb) * _dot(y_sgu, w_ob_ref[...]))
    o_ref[...] = x + _dot(merged.astype(BF16), w_out_ref[...])


def _ffn_kernel(x_ref, gain_ref, w_up_ref, cw_ref, cb_ref, w_down_ref, o_ref, zprev_ref):
    s_idx = pl.program_id(1)
    tile = x_ref.shape[0]

    @pl.when(s_idx == 0)
    def _():
        zprev_ref[...] = jnp.zeros_like(zprev_ref)

    x = x_ref[...]
    h = _rmsnorm_rows(x, gain_ref[...]).astype(BF16)
    row8 = lax.broadcasted_iota(jnp.int32, (SUBLANES, 1), 0)

    def up(j):
        return tuple(_dot(h, w_up_ref[:, c0:c0 + FF_CHUNK])
                     for c0 in (j * FF_CHUNK, D_FF + j * FF_CHUNK))

    def conv(z, col0):
        cols = slice(col0, col0 + FF_CHUNK)
        zp = zprev_ref[:, cols]
        zprev_ref[:, cols] = z[tile - SUBLANES:]
        z1 = pltpu.roll(z, 1, axis=0)
        z2 = pltpu.roll(z, 2, axis=0)
        top1 = jnp.where(row8 < 1, pltpu.roll(zp, 1, axis=0), z1[:SUBLANES])
        top2 = jnp.where(row8 < 2, pltpu.roll(zp, 2, axis=0), z2[:SUBLANES])
        z1 = jnp.concatenate([top1, z1[SUBLANES:]], axis=0)
        z2 = jnp.concatenate([top2, z2[SUBLANES:]], axis=0)
        cw = cw_ref[:, cols]
        return cb_ref[:, cols] + cw[0:1] * z2 + cw[1:2] * z1 + cw[2:3] * z

    def down(acc, act, j):
        return acc + _dot(act, w_down_ref[j * FF_CHUNK:(j + 1) * FF_CHUNK, :])

    acc = x
    n_chunks = D_FF // FF_CHUNK
    z_next = up(0)
    act_prev = None
    for j in range(n_chunks):
        zg, zv = z_next
        if j + 1 < n_chunks:
            z_next = up(j + 1)
        if act_prev is not None:
            acc = down(acc, act_prev, j - 1)
        gate = conv(zg, j * FF_CHUNK)
        val = conv(zv, D_FF + j * FF_CHUNK)
        act_prev = (jax.nn.silu(gate) * val).astype(BF16)
    acc = down(acc, act_prev, n_chunks - 1)
    o_ref[...] = acc


def _resident(shape, index_map):
    return pl.BlockSpec(shape, index_map, pipeline_mode=pl.Buffered(1))


def _mixer_call(layer, x, mix_norm, w_in, qg2, kg2, sinks, sgu_norm, w_s, b_exp, w_oa, w_ob, w_out):
    batch, seq, _ = x.shape
    grid = (batch, seq // SEQ_TILE)
    x_spec = pl.BlockSpec((None, SEQ_TILE, D_MODEL), lambda b, s: (b, s, 0))
    const3 = lambda b, s: (layer, 0, 0)
    const4 = lambda b, s: (layer, 0, 0, 0)
    return pl.pallas_call(
        functools.partial(_mixer_kernel, layer),
        grid=grid,
        in_specs=[
            x_spec,
            _resident((None, 1, D_MODEL), const3),
            _resident((None, D_MODEL, IN_WIDTH), const3),
            _resident((None, 1, LANES), const3),
            _resident((None, 1, LANES), const3),
            pl.BlockSpec(memory_space=pltpu.SMEM),
            _resident((None, 1, SGU_WIDTH), const3),
            _resident((None, SGU_GROUPS, BLOCK, BLOCK), const4),
            _resident((None, BLOCK, SGU_WIDTH), const3),
            _resident((None, ATT_WIDTH, D_MODEL), const3),
            _resident((None, SGU_WIDTH, D_MODEL), const3),
            _resident((None, D_MODEL, D_MODEL), const3),
        ],
        out_specs=x_spec,
        out_shape=jax.ShapeDtypeStruct(x.shape, x.dtype),
        scratch_shapes=[pltpu.VMEM((BLOCK, LANES), BF16)] * 4,
        compiler_params=pltpu.CompilerParams(
            dimension_semantics=("parallel", "arbitrary"),
            vmem_limit_bytes=VMEM_LIMIT_BYTES),
        name=f"mixer_l{layer}",
    )(x, mix_norm, w_in, qg2, kg2, sinks, sgu_norm, w_s, b_exp, w_oa, w_ob, w_out)


def _ffn_call(layer, x, ffn_norm, w_up, conv_w, conv_b, w_down):
    batch, seq, _ = x.shape
    grid = (batch, seq // SEQ_TILE)
    x_spec = pl.BlockSpec((None, SEQ_TILE, D_MODEL), lambda b, s: (b, s, 0))
    const3 = lambda b, s: (layer, 0, 0)
    return pl.pallas_call(
        _ffn_kernel,
        grid=grid,
        in_specs=[
            x_spec,
            _resident((None, 1, D_MODEL), const3),
            _resident((None, D_MODEL, 2 * D_FF), const3),
            _resident((None, CONV_WIDTH, 2 * D_FF), const3),
            _resident((None, 1, 2 * D_FF), const3),
            _resident((None, D_FF, D_MODEL), const3),
        ],
        out_specs=x_spec,
        out_shape=jax.ShapeDtypeStruct(x.shape, x.dtype),
        scratch_shapes=[pltpu.VMEM((SUBLANES, 2 * D_FF), F32)],
        compiler_params=pltpu.CompilerParams(
            dimension_semantics=("parallel", "arbitrary"),
            vmem_limit_bytes=VMEM_LIMIT_BYTES),
        name=f"ffn_l{layer}",
    )(x, ffn_norm, w_up, conv_w, conv_b, w_down)


def kernel(x, mix_norm, w_in, q_norm, k_norm, sinks, sgu_norm, w_s, b_s, w_oa, w_ob, w_out,
           ffn_norm, w_up, conv_w, conv_b, w_down):
    depth = w_in.shape[0]
    assert x.shape[1] % SEQ_TILE == 0 and SEQ_TILE % BLOCK == 0 and D_FF % FF_CHUNK == 0
    w_in_b, w_oa_b, w_ob_b, w_out_b, w_up_b, w_down_b = (
        w.astype(BF16) for w in (w_in, w_oa, w_ob, w_out, w_up, w_down))
    mix_norm3 = mix_norm[:, None, :]
    ffn_norm3 = ffn_norm[:, None, :]
    sgu_norm3 = sgu_norm[:, None, :]
    conv_b3 = conv_b[:, None, :]
    qg2 = jnp.tile(q_norm, (1, 2))[:, None, :]
    kg2 = jnp.tile(k_norm, (1, 2))[:, None, :]
    b_exp = jnp.repeat(jnp.swapaxes(b_s, 1, 2), SGU_WIDTH // SGU_GROUPS, axis=2)
    for layer in range(depth):
        x = _mixer_call(layer, x, mix_norm3, w_in_b, qg2, kg2, sinks, sgu_norm3, w_s, b_exp,
                        w_oa_b, w_ob_b, w_out_b)
        x = _ffn_call(layer, x, ffn_norm3, w_up_b, conv_w, conv_b3, w_down_b)
    return x
```

```python
import functools

import jax
import jax.numpy as jnp
from jax import lax
from jax.experimental import pallas as pl
from jax.experimental.pallas import tpu as pltpu

D_MODEL = 1024
N_Q_HEADS = 8
N_KV_HEADS = 2
HEAD_DIM = 64
ATT_WIDTH = N_Q_HEADS * HEAD_DIM
KV_WIDTH = N_KV_HEADS * HEAD_DIM
BLOCK = 128
SGU_WIDTH = D_MODEL // 2
SGU_GROUPS = 8
IN_WIDTH = ATT_WIDTH + 2 * KV_WIDTH + 2 * SGU_WIDTH + 2 * D_MODEL
D_FF = 2816
CONV_WIDTH = 3
NORM_EPS = 1e-6
NEG_INF = -1e30

LANES = 128
SUBLANES = 8
VMEM_LIMIT_BYTES = 56 * 1024 * 1024

SEQ_TILE = 512
SUB_TILE = 256
FF_CHUNK = 256

Q_OFF = 0
KV_OFF = ATT_WIDTH
SU_OFF = KV_OFF + 2 * KV_WIDTH
SV_OFF = SU_OFF + SGU_WIDTH
GA_OFF = SV_OFF + SGU_WIDTH
GB_OFF = GA_OFF + D_MODEL

BF16 = jnp.bfloat16
F32 = jnp.float32


def _dot(a, b):
    return jnp.dot(a, b, preferred_element_type=F32)


def _dot_nt(a, b):
    return lax.dot_general(a, b, (((1,), (1,)), ((), ())), preferred_element_type=F32)


def _rmsnorm_rows(x, gain):
    ms = jnp.mean(x * x, axis=-1, keepdims=True)
    return x * lax.rsqrt(ms + NORM_EPS) * gain


def _half_rmsnorm(t, gain2, lo):
    t2 = t * t
    s_lo = jnp.sum(jnp.where(lo, t2, 0.0), axis=-1, keepdims=True)
    s_hi = jnp.sum(jnp.where(lo, 0.0, t2), axis=-1, keepdims=True)
    ms = jnp.where(lo, s_lo, s_hi) * (1.0 / HEAD_DIM)
    return t * lax.rsqrt(ms + NORM_EPS) * gain2


def _mixer_kernel(layer, x_ref, gain_ref, w_in_ref, qg_ref, kg_ref, sinks_ref, sgu_gain_ref,
                  ws_ref, bexp_ref, w_oa_ref, w_ob_ref, w_out_ref, o_ref,
                  kprev_ref, kprev_sw_ref, vprev_ref, vprev_sw_ref):
    s_idx = pl.program_id(1)

    @pl.when(s_idx == 0)
    def _():
        kprev_ref[...] = jnp.zeros_like(kprev_ref)
        kprev_sw_ref[...] = jnp.zeros_like(kprev_sw_ref)
        vprev_ref[...] = jnp.zeros_like(vprev_ref)
        vprev_sw_ref[...] = jnp.zeros_like(vprev_sw_ref)

    first_col = jnp.where(s_idx > 0, 0, BLOCK)
    for i in range(x_ref.shape[0] // SUB_TILE):
        rows = pl.ds(i * SUB_TILE, SUB_TILE)
        _mixer_rows(layer, first_col if i == 0 else None, x_ref.at[rows], gain_ref, w_in_ref, qg_ref,
                    kg_ref, sinks_ref, sgu_gain_ref, ws_ref, bexp_ref, w_oa_ref, w_ob_ref, w_out_ref,
                    o_ref.at[rows], kprev_ref, kprev_sw_ref, vprev_ref, vprev_sw_ref)


def _mixer_rows(layer, first_col, x_ref, gain_ref, w_in_ref, qg_ref, kg_ref, sinks_ref, sgu_gain_ref,
                ws_ref, bexp_ref, w_oa_ref, w_ob_ref, w_out_ref, o_ref,
                kprev_ref, kprev_sw_ref, vprev_ref, vprev_sw_ref):
    tile = x_ref.shape[0]
    n_blocks = tile // BLOCK
    x = x_ref[...]
    h = _rmsnorm_rows(x, gain_ref[...]).astype(BF16)

    lane = lax.broadcasted_iota(jnp.int32, (1, LANES), 1)
    lo = lane < HEAD_DIM

    def proj(off, width):
        return _dot(h, w_in_ref[:, off:off + width])

    q = proj(Q_OFF, ATT_WIDTH)
    kv = proj(KV_OFF, 2 * KV_WIDTH)
    su = proj(SU_OFF, SGU_WIDTH)
    sv = proj(SV_OFF, SGU_WIDTH)

    k = _half_rmsnorm(kv[:, :KV_WIDTH], kg_ref[...], lo)
    v = kv[:, KV_WIDTH:]
    k_sw = pltpu.roll(k, HEAD_DIM, axis=1)
    v_sw = pltpu.roll(v, HEAD_DIM, axis=1)
    k_b, k_sw_b, v_b, v_sw_b = (t.astype(BF16) for t in (k, k_sw, v, v_sw))

    scale = HEAD_DIM ** -0.5
    qg = qg_ref[...] * scale
    q_lo, q_hi = [], []
    for c in range(ATT_WIDTH // LANES):
        qc = _half_rmsnorm(q[:, c * LANES:(c + 1) * LANES], qg, lo)
        q_lo.append(jnp.where(lo, qc, 0.0).astype(BF16))
        q_hi.append(jnp.where(lo, 0.0, qc).astype(BF16))

    qi = lax.broadcasted_iota(jnp.int32, (BLOCK, 2 * BLOCK), 0)
    kj = lax.broadcasted_iota(jnp.int32, (BLOCK, 2 * BLOCK), 1)
    dist = qi + BLOCK - kj
    in_window = (dist >= 0) & (dist < BLOCK)
    dist_f = dist.astype(F32)

    def band(cur, prev_ref, n):
        prev = prev_ref[...] if n == 0 else cur[(n - 1) * BLOCK:n * BLOCK]
        return jnp.concatenate([prev, cur[n * BLOCK:(n + 1) * BLOCK]], axis=0)

    def qk(n):
        rows = slice(n * BLOCK, (n + 1) * BLOCK)
        q_same = jnp.concatenate([q_lo[0][rows], q_lo[1][rows], q_hi[2][rows], q_hi[3][rows]], axis=0)
        q_swap = jnp.concatenate([q_hi[0][rows], q_hi[1][rows], q_lo[2][rows], q_lo[3][rows]], axis=0)
        s_same = _dot_nt(q_same, band(k_b, kprev_ref, n))
        s_swap = _dot_nt(q_swap, band(k_sw_b, kprev_sw_ref, n))
        return {0: s_same[0:128], 2: s_same[128:256], 5: s_same[256:384], 7: s_same[384:512],
                1: s_swap[0:128], 3: s_swap[128:256], 4: s_swap[256:384], 6: s_swap[384:512]}

    def softmax(n, head_scores):
        valid = in_window & (kj >= first_col) if (n == 0 and first_col is not None) else in_window
        probs, inv = {}, {}
        for hd in range(N_Q_HEADS):
            slope = 2.0 ** (-(hd + 1))
            sc = jnp.where(valid, head_scores[hd] - slope * dist_f, NEG_INF)
            sink = sinks_ref[layer, hd]
            m = jnp.maximum(jnp.max(sc, axis=-1, keepdims=True), sink)
            p = jnp.exp(sc - m)
            denom = jnp.sum(p, axis=-1, keepdims=True) + jnp.exp(sink - m)
            probs[hd] = p.astype(BF16)
            inv[hd] = 1.0 / denom
        return probs, inv

    def pv(n, probs, inv):
        vb = band(v_b, vprev_ref, n)
        vb_sw = band(v_sw_b, vprev_sw_ref, n)
        zero = jnp.zeros_like(vb)
        vb_lo, vb_hi = jnp.where(lo, vb, zero), jnp.where(lo, zero, vb)
        vsw_lo, vsw_hi = jnp.where(lo, vb_sw, zero), jnp.where(lo, zero, vb_sw)
        cols = []
        for c in range(ATT_WIDTH // LANES):
            if c < 2:
                o = _dot(probs[2 * c], vb_lo) + _dot(probs[2 * c + 1], vsw_hi)
            else:
                o = _dot(probs[2 * c], vsw_lo) + _dot(probs[2 * c + 1], vb_hi)
            cols.append(o * jnp.where(lo, inv[2 * c], inv[2 * c + 1]))
        return jnp.concatenate(cols, axis=1)

    gate_proj = [GA_OFF, GB_OFF]
    gates = []
    scores = {0: qk(0)}
    soft = {}
    y_att_blocks = []
    for n in range(n_blocks):
        if n + 1 < n_blocks:
            scores[n + 1] = qk(n + 1)
        if len(gates) < len(gate_proj):
            gates.append(proj(gate_proj[len(gates)], D_MODEL))
        if n > 0:
            y_att_blocks.append(pv(n - 1, *soft.pop(n - 1)))
        soft[n] = softmax(n, scores.pop(n))
    while len(gates) < len(gate_proj):
        gates.append(proj(gate_proj[len(gates)], D_MODEL))
    ga, gb = gates
    y_att_blocks.append(pv(n_blocks - 1, *soft.pop(n_blocks - 1)))

    kprev_ref[...] = k_b[tile - BLOCK:]
    kprev_sw_ref[...] = k_sw_b[tile - BLOCK:]
    vprev_ref[...] = v_b[tile - BLOCK:]
    vprev_sw_ref[...] = v_sw_b[tile - BLOCK:]
    y_att = jnp.concatenate(y_att_blocks, axis=0).astype(BF16)

    u = jax.nn.gelu(su)
    vn = _rmsnorm_rows(jax.nn.gelu(sv), sgu_gain_ref[...])
    ci = lax.broadcasted_iota(jnp.int32, (BLOCK, BLOCK), 0)
    cj = lax.broadcasted_iota(jnp.int32, (BLOCK, BLOCK), 1)
    causal = cj <= ci
    w_tril = [jnp.where(causal, ws_ref[g], 0.0).astype(BF16) for g in range(SGU_GROUPS)]
    bexp = bexp_ref[...]
    sgu_blocks = []
    for n in range(n_blocks):
        rows = slice(n * BLOCK, (n + 1) * BLOCK)
        cols = []
        for c in range(SGU_WIDTH // LANES):
            vc = vn[rows, c * LANES:(c + 1) * LANES]
            v_lo = jnp.where(lo, vc, 0.0).astype(BF16)
            v_hi = jnp.where(lo, 0.0, vc).astype(BF16)
            cols.append(_dot(w_tril[2 * c], v_lo) + _dot(w_tril[2 * c + 1], v_hi))
        mixed = jnp.concatenate(cols, axis=1) + bexp
        sgu_blocks.append(u[rows] * mixed)
    y_sgu = jnp.concatenate(sgu_blocks, axis=0).astype(BF16)

    merged = (jax.nn.sigmoid(ga) * _dot(y_att, w_oa_ref[...])
              + jax.nn.sigmoid(gb) * _dot(y_sgu, w_ob_ref[...]))
    o_ref[...] = x + _dot(merged.astype(BF16), w_out_ref[...])


def _ffn_kernel(x_ref, gain_ref, w_up_ref, cw_ref, cb_ref, w_down_ref, o_ref, zprev_ref):
    @pl.when(pl.program_id(1) == 0)
    def _():
        zprev_ref[...] = jnp.zeros_like(zprev_ref)

    for i in range(x_ref.shape[0] // SUB_TILE):
        rows = pl.ds(i * SUB_TILE, SUB_TILE)
        _ffn_rows(x_ref.at[rows], gain_ref, w_up_ref, cw_ref, cb_ref, w_down_ref, o_ref.at[rows], zprev_ref)


def _ffn_rows(x_ref, gain_ref, w_up_ref, cw_ref, cb_ref, w_down_ref, o_ref, zprev_ref):
    tile = x_ref.shape[0]
    x = x_ref[...]
    h = _rmsnorm_rows(x, gain_ref[...]).astype(BF16)
    row8 = lax.broadcasted_iota(jnp.int32, (SUBLANES, 1), 0)

    def up(j):
        return tuple(_dot(h, w_up_ref[:, c0:c0 + FF_CHUNK])
                     for c0 in (j * FF_CHUNK, D_FF + j * FF_CHUNK))

    def conv(z, col0):
        cols = slice(col0, col0 + FF_CHUNK)
        zp = zprev_ref[:, cols]
        zprev_ref[:, cols] = z[tile - SUBLANES:]
        z1 = pltpu.roll(z, 1, axis=0)
        z2 = pltpu.roll(z, 2, axis=0)
        top1 = jnp.where(row8 < 1, pltpu.roll(zp, 1, axis=0), z1[:SUBLANES])
        top2 = jnp.where(row8 < 2, pltpu.roll(zp, 2, axis=0), z2[:SUBLANES])
        z1 = jnp.concatenate([top1, z1[SUBLANES:]], axis=0)
        z2 = jnp.concatenate([top2, z2[SUBLANES:]], axis=0)
        cw = cw_ref[:, cols]
        return cb_ref[:, cols] + cw[0:1] * z2 + cw[1:2] * z1 + cw[2:3] * z

    def down(acc, act, j):
        return acc + _dot(act, w_down_ref[j * FF_CHUNK:(j + 1) * FF_CHUNK, :])

    acc = x
    n_chunks = D_FF // FF_CHUNK
    z_next = up(0)
    act_prev = None
    for j in range(n_chunks):
        zg, zv = z_next
        if j + 1 < n_chunks:
            z_next = up(j + 1)
        if act_prev is not None:
            acc = down(acc, act_prev, j - 1)
        gate = conv(zg, j * FF_CHUNK)
        val = conv(zv, D_FF + j * FF_CHUNK)
        act_prev = (jax.nn.silu(gate) * val).astype(BF16)
    acc = down(acc, act_prev, n_chunks - 1)
    o_ref[...] = acc


def _resident(shape, index_map):
    return pl.BlockSpec(shape, index_map, pipeline_mode=pl.Buffered(1))


def _mixer_call(layer, x, mix_norm, w_in, qg2, kg2, sinks, sgu_norm, w_s, b_exp, w_oa, w_ob, w_out):
    batch, seq, _ = x.shape
    grid = (batch, seq // SEQ_TILE)
    x_spec = pl.BlockSpec((None, SEQ_TILE, D_MODEL), lambda b, s: (b, s, 0))
    const3 = lambda b, s: (layer, 0, 0)
    const4 = lambda b, s: (layer, 0, 0, 0)
    return pl.pallas_call(
        functools.partial(_mixer_kernel, layer),
        grid=grid,
        in_specs=[
            x_spec,
            _resident((None, 1, D_MODEL), const3),
            _resident((None, D_MODEL, IN_WIDTH), const3),
            _resident((None, 1, LANES), const3),
            _resident((None, 1, LANES), const3),
            pl.BlockSpec(memory_space=pltpu.SMEM),
            _resident((None, 1, SGU_WIDTH), const3),
            _resident((None, SGU_GROUPS, BLOCK, BLOCK), const4),
            _resident((None, BLOCK, SGU_WIDTH), const3),
            _resident((None, ATT_WIDTH, D_MODEL), const3),
            _resident((None, SGU_WIDTH, D_MODEL), const3),
            _resident((None, D_MODEL, D_MODEL), const3),
        ],
        out_specs=x_spec,
        out_shape=jax.ShapeDtypeStruct(x.shape, x.dtype),
        scratch_shapes=[pltpu.VMEM((BLOCK, LANES), BF16)] * 4,
        compiler_params=pltpu.CompilerParams(
            dimension_semantics=("parallel", "arbitrary"),
            vmem_limit_bytes=VMEM_LIMIT_BYTES),
        name=f"mixer_l{layer}",
    )(x, mix_norm, w_in, qg2, kg2, sinks, sgu_norm, w_s, b_exp, w_oa, w_ob, w_out)


def _ffn_call(layer, x, ffn_norm, w_up, conv_w, conv_b, w_down):
    batch, seq, _ = x.shape
    grid = (batch, seq // SEQ_TILE)
    x_spec = pl.BlockSpec((None, SEQ_TILE, D_MODEL), lambda b, s: (b, s, 0))
    const3 = lambda b, s: (layer, 0, 0)
    return pl.pallas_call(
        _ffn_kernel,
        grid=grid,
        in_specs=[
            x_spec,
            _resident((None, 1, D_MODEL), const3),
            _resident((None, D_MODEL, 2 * D_FF), const3),
            _resident((None, CONV_WIDTH, 2 * D_FF), const3),
            _resident((None, 1, 2 * D_FF), const3),
            _resident((None, D_FF, D_MODEL), const3),
        ],
        out_specs=x_spec,
        out_shape=jax.ShapeDtypeStruct(x.shape, x.dtype),
        scratch_shapes=[pltpu.VMEM((SUBLANES, 2 * D_FF), F32)],
        compiler_params=pltpu.CompilerParams(
            dimension_semantics=("parallel", "arbitrary"),
            vmem_limit_bytes=VMEM_LIMIT_BYTES),
        name=f"ffn_l{layer}",
    )(x, ffn_norm, w_up, conv_w, conv_b, w_down)


def kernel(x, mix_norm, w_in, q_norm, k_norm, sinks, sgu_norm, w_s, b_s, w_oa, w_ob, w_out,
           ffn_norm, w_up, conv_w, conv_b, w_down):
    depth = w_in.shape[0]
    assert x.shape[1] % SEQ_TILE == 0 and SEQ_TILE % BLOCK == 0 and D_FF % FF_CHUNK == 0
    w_in_b, w_oa_b, w_ob_b, w_out_b, w_up_b, w_down_b = (
        w.astype(BF16) for w in (w_in, w_oa, w_ob, w_out, w_up, w_down))
    mix_norm3 = mix_norm[:, None, :]
    ffn_norm3 = ffn_norm[:, None, :]
    sgu_norm3 = sgu_norm[:, None, :]
    conv_b3 = conv_b[:, None, :]
    qg2 = jnp.tile(q_norm, (1, 2))[:, None, :]
    kg2 = jnp.tile(k_norm, (1, 2))[:, None, :]
    b_exp = jnp.repeat(jnp.swapaxes(b_s, 1, 2), SGU_WIDTH // SGU_GROUPS, axis=2)
    for layer in range(depth):
        x = _mixer_call(layer, x, mix_norm3, w_in_b, qg2, kg2, sinks, sgu_norm3, w_s, b_exp,
                        w_oa_b, w_ob_b, w_out_b)
        x = _ffn_call(layer, x, ffn_norm3, w_up_b, conv_w, conv_b3, w_down_b)
    return x
```

```python
import functools

import jax
import jax.numpy as jnp
from jax import lax
from jax.experimental import pallas as pl
from jax.experimental.pallas import tpu as pltpu

D_MODEL = 1024
N_Q_HEADS = 8
N_KV_HEADS = 2
HEAD_DIM = 64
ATT_WIDTH = N_Q_HEADS * HEAD_DIM
KV_WIDTH = N_KV_HEADS * HEAD_DIM
BLOCK = 128
SGU_WIDTH = D_MODEL // 2
SGU_GROUPS = 8
IN_WIDTH = ATT_WIDTH + 2 * KV_WIDTH + 2 * SGU_WIDTH + 2 * D_MODEL
D_FF = 2816
CONV_WIDTH = 3
NORM_EPS = 1e-6
NEG_INF = -1e30

LANES = 128
SUBLANES = 8
VMEM_LIMIT_BYTES = 56 * 1024 * 1024

SEQ_TILE = 512
SUB_TILE = 256
FF_CHUNK = 256

Q_OFF = 0
KV_OFF = ATT_WIDTH
SU_OFF = KV_OFF + 2 * KV_WIDTH
SV_OFF = SU_OFF + SGU_WIDTH
GA_OFF = SV_OFF + SGU_WIDTH
GB_OFF = GA_OFF + D_MODEL

BF16 = jnp.bfloat16
F32 = jnp.float32


def _dot(a, b):
    return jnp.dot(a, b, preferred_element_type=F32)


def _dot_nt(a, b):
    return lax.dot_general(a, b, (((1,), (1,)), ((), ())), preferred_element_type=F32)


def _rmsnorm_rows(x, gain):
    ms = jnp.mean(x * x, axis=-1, keepdims=True)
    return x * lax.rsqrt(ms + NORM_EPS) * gain


def _half_rmsnorm(t, gain2, lo):
    t2 = t * t
    s_lo = jnp.sum(jnp.where(lo, t2, 0.0), axis=-1, keepdims=True)
    s_hi = jnp.sum(jnp.where(lo, 0.0, t2), axis=-1, keepdims=True)
    ms = jnp.where(lo, s_lo, s_hi) * (1.0 / HEAD_DIM)
    return t * lax.rsqrt(ms + NORM_EPS) * gain2


def _mixer_kernel(layer, x_ref, gain_ref, w_in_ref, qg_ref, kg_ref, sinks_ref, sgu_gain_ref,
                  ws_ref, bexp_ref, w_oa_ref, w_ob_ref, w_out_ref, o_ref,
                  kprev_ref, kprev_sw_ref, vprev_ref, vprev_sw_ref):
    s_idx = pl.program_id(1)

    @pl.when(s_idx == 0)
    def _():
        kprev_ref[...] = jnp.zeros_like(kprev_ref)
        kprev_sw_ref[...] = jnp.zeros_like(kprev_sw_ref)
        vprev_ref[...] = jnp.zeros_like(vprev_ref)
        vprev_sw_ref[...] = jnp.zeros_like(vprev_sw_ref)

    first_col = jnp.where(s_idx > 0, 0, BLOCK)
    for i in range(x_ref.shape[0] // SUB_TILE):
        rows = pl.ds(i * SUB_TILE, SUB_TILE)
        _mixer_rows(layer, first_col if i == 0 else None, x_ref.at[rows], gain_ref, w_in_ref, qg_ref,
                    kg_ref, sinks_ref, sgu_gain_ref, ws_ref, bexp_ref, w_oa_ref, w_ob_ref, w_out_ref,
                    o_ref.at[rows], kprev_ref, kprev_sw_ref, vprev_ref, vprev_sw_ref)


def _mixer_rows(layer, first_col, x_ref, gain_ref, w_in_ref, qg_ref, kg_ref, sinks_ref, sgu_gain_ref,
                ws_ref, bexp_ref, w_oa_ref, w_ob_ref, w_out_ref, o_ref,
                kprev_ref, kprev_sw_ref, vprev_ref, vprev_sw_ref):
    tile = x_ref.shape[0]
    n_blocks = tile // BLOCK
    x = x_ref[...]
    h = _rmsnorm_rows(x, gain_ref[...]).astype(BF16)

    lane = lax.broadcasted_iota(jnp.int32, (1, LANES), 1)
    lo = lane < HEAD_DIM

    def proj(off, width):
        return _dot(h, w_in_ref[:, off:off + width])

    q = proj(Q_OFF, ATT_WIDTH)
    kv = proj(KV_OFF, 2 * KV_WIDTH)
    su = proj(SU_OFF, SGU_WIDTH)
    sv = proj(SV_OFF, SGU_WIDTH)

    k = _half_rmsnorm(kv[:, :KV_WIDTH], kg_ref[...], lo)
    v = kv[:, KV_WIDTH:]
    k_sw = pltpu.roll(k, HEAD_DIM, axis=1)
    v_sw = pltpu.roll(v, HEAD_DIM, axis=1)
    k_b, k_sw_b, v_b, v_sw_b = (t.astype(BF16) for t in (k, k_sw, v, v_sw))

    scale = HEAD_DIM ** -0.5
    qg = qg_ref[...] * scale
    q_lo, q_hi = [], []
    for c in range(ATT_WIDTH // LANES):
        qc = _half_rmsnorm(q[:, c * LANES:(c + 1) * LANES], qg, lo)
        q_lo.append(jnp.where(lo, qc, 0.0).astype(BF16))
        q_hi.append(jnp.where(lo, 0.0, qc).astype(BF16))

    qi = lax.broadcasted_iota(jnp.int32, (BLOCK, 2 * BLOCK), 0)
    kj = lax.broadcasted_iota(jnp.int32, (BLOCK, 2 * BLOCK), 1)
    dist = qi + BLOCK - kj
    in_window = (dist >= 0) & (dist < BLOCK)
    dist_f = dist.astype(F32)

    def band(cur, prev_ref, n):
        prev = prev_ref[...] if n == 0 else cur[(n - 1) * BLOCK:n * BLOCK]
        return jnp.concatenate([prev, cur[n * BLOCK:(n + 1) * BLOCK]], axis=0)

    def qk(n):
        rows = slice(n * BLOCK, (n + 1) * BLOCK)
        q_same = jnp.concatenate([q_lo[0][rows], q_lo[1][rows], q_hi[2][rows], q_hi[3][rows]], axis=0)
        q_swap = jnp.concatenate([q_hi[0][rows], q_hi[1][rows], q_lo[2][rows], q_lo[3][rows]], axis=0)
        s_same = _dot_nt(q_same, band(k_b, kprev_ref, n))
        s_swap = _dot_nt(q_swap, band(k_sw_b, kprev_sw_ref, n))
        return {0: s_same[0:128], 2: s_same[128:256], 5: s_same[256:384], 7: s_same[384:512],
                1: s_swap[0:128], 3: s_swap[128:256], 4: s_swap[256:384], 6: s_swap[384:512]}

    def softmax(n, head_scores):
        valid = in_window & (kj >= first_col) if (n == 0 and first_col is not None) else in_window
        probs, inv = {}, {}
        for hd in range(N_Q_HEADS):
            slope = 2.0 ** (-(hd + 1))
            sc = jnp.where(valid, head_scores[hd] - slope * dist_f, NEG_INF)
            sink = sinks_ref[layer, hd]
            m = jnp.maximum(jnp.max(sc, axis=-1, keepdims=True), sink)
            p = jnp.exp(sc - m)
            denom = jnp.sum(p, axis=-1, keepdims=True) + jnp.exp(sink - m)
            probs[hd] = p.astype(BF16)
            inv[hd] = 1.0 / denom
        return probs, inv

    def pv(n, probs, inv):
        vb = band(v_b, vprev_ref, n)
        vb_sw = band(v_sw_b, vprev_sw_ref, n)
        zero = jnp.zeros_like(vb)
        vb_lo, vb_hi = jnp.where(lo, vb, zero), jnp.where(lo, zero, vb)
        vsw_lo, vsw_hi = jnp.where(lo, vb_sw, zero), jnp.where(lo, zero, vb_sw)
        cols = []
        for c in range(ATT_WIDTH // LANES):
            if c < 2:
                o = _dot(probs[2 * c], vb_lo) + _dot(probs[2 * c + 1], vsw_hi)
            else:
                o = _dot(probs[2 * c], vsw_lo) + _dot(probs[2 * c + 1], vb_hi)
            cols.append(o * jnp.where(lo, inv[2 * c], inv[2 * c + 1]))
        return jnp.concatenate(cols, axis=1)

    gate_proj = [GA_OFF, GB_OFF]
    gates = []
    scores = {0: qk(0)}
    soft = {}
    y_att_blocks = []
    for n in range(n_blocks):
        if n + 1 < n_blocks:
            scores[n + 1] = qk(n + 1)
        if len(gates) < len(gate_proj):
            gates.append(proj(gate_proj[len(gates)], D_MODEL))
        if n > 0:
            y_att_blocks.append(pv(n - 1, *soft.pop(n - 1)))
        soft[n] = softmax(n, scores.pop(n))
    while len(gates) < len(gate_proj):
        gates.append(proj(gate_proj[len(gates)], D_MODEL))
    ga, gb = gates
    y_att_blocks.append(pv(n_blocks - 1, *soft.pop(n_blocks - 1)))

    kprev_ref[...] = k_b[tile - BLOCK:]
    kprev_sw_ref[...] = k_sw_b[tile - BLOCK:]
    vprev_ref[...] = v_b[tile - BLOCK:]
    vprev_sw_ref[...] = v_sw_b[tile - BLOCK:]
    y_att = jnp.concatenate(y_att_blocks, axis=0).astype(BF16)

    u = jax.nn.gelu(su)
    vn = _rmsnorm_rows(jax.nn.gelu(sv), sgu_gain_ref[...])
    ci = lax.broadcasted_iota(jnp.int32, (BLOCK, BLOCK), 0)
    cj = lax.broadcasted_iota(jnp.int32, (BLOCK, BLOCK), 1)
    causal = cj <= ci
    w_tril = [jnp.where(causal, ws_ref[g], 0.0).astype(BF16) for g in range(SGU_GROUPS)]
    bexp = bexp_ref[...]
    sgu_blocks = []
    for n in range(n_blocks):
        rows = slice(n * BLOCK, (n + 1) * BLOCK)
        cols = []
        for c in range(SGU_WIDTH // LANES):
            vc = vn[rows, c * LANES:(c + 1) * LANES]
            v_lo = jnp.where(lo, vc, 0.0).astype(BF16)
            v_hi = jnp.where(lo, 0.0, vc).astype(BF16)
            cols.append(_dot(w_tril[2 * c], v_lo) + _dot(w_tril[2 * c + 1], v_hi))
        mixed = jnp.concatenate(cols, axis=1) + bexp
        sgu_blocks.append(u[rows] * mixed)
    y_sgu = jnp.concatenate(sgu_blocks, axis=0).astype(BF16)

    merged = (jax.nn.sigmoid(ga) * _dot(y_att, w_oa_ref[...])
              + jax.nn.sigmoid(gb) * _dot(y_sgu, w_ob_ref[...]))
    o_ref[...] = x + _dot(merged.astype(BF16), w_out_ref[...])


def _ffn_kernel(x_ref, gain_ref, w_up_ref, cw_ref, cb_ref, w_down_ref, o_ref, zs_ref):
    @pl.when(pl.program_id(1) == 0)
    def _():
        zs_ref[:, pl.ds(0, SUBLANES), :] = jnp.zeros((zs_ref.shape[0], SUBLANES, LANES), F32)

    for i in range(x_ref.shape[0] // SUB_TILE):
        rows = pl.ds(i * SUB_TILE, SUB_TILE)
        _ffn_rows(x_ref.at[rows], gain_ref, w_up_ref, cw_ref, cb_ref, w_down_ref, o_ref.at[rows], zs_ref)


def _ffn_rows(x_ref, gain_ref, w_up_ref, cw_ref, cb_ref, w_down_ref, o_ref, zs_ref):
    tile = x_ref.shape[0]
    x = x_ref[...]
    h = _rmsnorm_rows(x, gain_ref[...]).astype(BF16)
    tiles_per_chunk = FF_CHUNK // LANES

    def up(j):
        for c0 in (j * FF_CHUNK, D_FF + j * FF_CHUNK):
            z = _dot(h, w_up_ref[:, c0:c0 + FF_CHUNK])
            for t in range(tiles_per_chunk):
                zs_ref[c0 // LANES + t, pl.ds(SUBLANES, tile), :] = z[:, t * LANES:(t + 1) * LANES]

    def conv(col0):
        out = []
        for t in range(tiles_per_chunk):
            c = col0 // LANES + t
            cols = slice(c * LANES, (c + 1) * LANES)
            z0 = zs_ref[c, pl.ds(SUBLANES, tile), :]
            z1 = zs_ref[c, pl.ds(SUBLANES - 1, tile), :]
            z2 = zs_ref[c, pl.ds(SUBLANES - 2, tile), :]
            zs_ref[c, pl.ds(0, SUBLANES), :] = zs_ref[c, pl.ds(tile, SUBLANES), :]
            cw = cw_ref[:, cols]
            out.append(cb_ref[:, cols] + cw[0:1] * z2 + cw[1:2] * z1 + cw[2:3] * z0)
        return jnp.concatenate(out, axis=1)

    def down(acc, act, j):
        return acc + _dot(act, w_down_ref[j * FF_CHUNK:(j + 1) * FF_CHUNK, :])

    acc = x
    n_chunks = D_FF // FF_CHUNK
    up(0)
    act_prev = None
    for j in range(n_chunks):
        if j + 1 < n_chunks:
            up(j + 1)
        if act_prev is not None:
            acc = down(acc, act_prev, j - 1)
        gate = conv(j * FF_CHUNK)
        val = conv(D_FF + j * FF_CHUNK)
        act_prev = (jax.nn.silu(gate) * val).astype(BF16)
    acc = down(acc, act_prev, n_chunks - 1)
    o_ref[...] = acc


def _resident(shape, index_map):
    return pl.BlockSpec(shape, index_map, pipeline_mode=pl.Buffered(1))


def _mixer_call(layer, x, mix_norm, w_in, qg2, kg2, sinks, sgu_norm, w_s, b_exp, w_oa, w_ob, w_out):
    batch, seq, _ = x.shape
    grid = (batch, seq // SEQ_TILE)
    x_spec = pl.BlockSpec((None, SEQ_TILE, D_MODEL), lambda b, s: (b, s, 0))
    const3 = lambda b, s: (layer, 0, 0)
    const4 = lambda b, s: (layer, 0, 0, 0)
    return pl.pallas_call(
        functools.partial(_mixer_kernel, layer),
        grid=grid,
        in_specs=[
            x_spec,
            _resident((None, 1, D_MODEL), const3),
            _resident((None, D_MODEL, IN_WIDTH), const3),
            _resident((None, 1, LANES), const3),
            _resident((None, 1, LANES), const3),
            pl.BlockSpec(memory_space=pltpu.SMEM),
            _resident((None, 1, SGU_WIDTH), const3),
            _resident((None, SGU_GROUPS, BLOCK, BLOCK), const4),
            _resident((None, BLOCK, SGU_WIDTH), const3),
            _resident((None, ATT_WIDTH, D_MODEL), const3),
            _resident((None, SGU_WIDTH, D_MODEL), const3),
            _resident((None, D_MODEL, D_MODEL), const3),
        ],
        out_specs=x_spec,
        out_shape=jax.ShapeDtypeStruct(x.shape, x.dtype),
        scratch_shapes=[pltpu.VMEM((BLOCK, LANES), BF16)] * 4,
        compiler_params=pltpu.CompilerParams(
            dimension_semantics=("parallel", "arbitrary"),
            vmem_limit_bytes=VMEM_LIMIT_BYTES),
        name=f"mixer_l{layer}",
    )(x, mix_norm, w_in, qg2, kg2, sinks, sgu_norm, w_s, b_exp, w_oa, w_ob, w_out)


def _ffn_call(layer, x, ffn_norm, w_up, conv_w, conv_b, w_down):
    batch, seq, _ = x.shape
    grid = (batch, seq // SEQ_TILE)
    x_spec = pl.BlockSpec((None, SEQ_TILE, D_MODEL), lambda b, s: (b, s, 0))
    const3 = lambda b, s: (layer, 0, 0)
    return pl.pallas_call(
        _ffn_kernel,
        grid=grid,
        in_specs=[
            x_spec,
            _resident((None, 1, D_MODEL), const3),
            _resident((None, D_MODEL, 2 * D_FF), const3),
            _resident((None, CONV_WIDTH, 2 * D_FF), const3),
            _resident((None, 1, 2 * D_FF), const3),
            _resident((None, D_FF, D_MODEL), const3),
        ],
        out_specs=x_spec,
        out_shape=jax.ShapeDtypeStruct(x.shape, x.dtype),
        scratch_shapes=[pltpu.VMEM((2 * D_FF // LANES, SUB_TILE + SUBLANES, LANES), F32)],
        compiler_params=pltpu.CompilerParams(
            dimension_semantics=("parallel", "arbitrary"),
            vmem_limit_bytes=VMEM_LIMIT_BYTES),
        name=f"ffn_l{layer}",
    )(x, ffn_norm, w_up, conv_w, conv_b, w_down)


def kernel(x, mix_norm, w_in, q_norm, k_norm, sinks, sgu_norm, w_s, b_s, w_oa, w_ob, w_out,
           ffn_norm, w_up, conv_w, conv_b, w_down):
    depth = w_in.shape[0]
    assert x.shape[1] % SEQ_TILE == 0 and SEQ_TILE % BLOCK == 0 and D_FF % FF_CHUNK == 0
    w_in_b, w_oa_b, w_ob_b, w_out_b, w_up_b, w_down_b = (
        w.astype(BF16) for w in (w_in, w_oa, w_ob, w_out, w_up, w_down))
    mix_norm3 = mix_norm[:, None, :]
    ffn_norm3 = ffn_norm[:, None, :]
    sgu_norm3 = sgu_norm[:, None, :]
    conv_b3 = conv_b[:, None, :]
    qg2 = jnp.tile(q_norm, (1, 2))[:, None, :]
    kg2 = jnp.tile(k_norm, (1, 2))[:, None, :]
    b_exp = jnp.repeat(jnp.swapaxes(b_s, 1, 2), SGU_WIDTH // SGU_GROUPS, axis=2)
    for layer in range(depth):
        x = _mixer_call(layer, x, mix_norm3, w_in_b, qg2, kg2, sinks, sgu_norm3, w_s, b_exp,
                        w_oa_b, w_ob_b, w_out_b)
        x = _ffn_call(layer, x, ffn_norm3, w_up_b, conv_w, conv_b3, w_down_b)
    return x
```
